```python
import math
import jax, jax.numpy as jnp
from jax import lax
import numpy as np

D_MODEL = 4096
BATCH = 1
SEQ = 8192
DEPTH = 2

HEAD_DIM = 128
ROPE_THETA = 10000.0
RMS_EPS = 1e-6
SUBLN_EPS = 1e-5

A_HEADS = D_MODEL // (2 * HEAD_DIM)
A_KV_HEADS = A_HEADS // 4
A_WINDOW = 128
A_BLOCK = 128

B_HEADS = D_MODEL // (2 * HEAD_DIM)
GRID_W = 64
NA_ROWS_MAX = 8
NA_COLS = 16

C_HEADS = D_MODEL // (2 * HEAD_DIM)
C_DIM = HEAD_DIM
C_BLOCK = 128

N_EVEN = (DEPTH + 1) // 2
N_ODD = DEPTH // 2

A_Q = A_HEADS * HEAD_DIM
A_KV = A_KV_HEADS * HEAD_DIM
B_W = B_HEADS * HEAD_DIM
EVEN_SPLITS = [A_Q, A_KV, A_KV, A_Q, B_W, B_W, B_W, B_W]
EVEN_IN = sum(EVEN_SPLITS)
EVEN_MIX = A_Q + B_W
C_W = C_HEADS * 2 * C_DIM
ODD_IN = 4 * C_W

kernel_name = "hybrid_window_natten_diffattn_encoder"


def _offsets(sizes):
    out, acc = [], 0
    for s in sizes[:-1]:
        acc += s
        out.append(acc)
    return out


def rms_norm(x, g, eps=RMS_EPS):
    xf = x.astype(jnp.float32)
    y = xf * lax.rsqrt(jnp.mean(xf * xf, axis=-1, keepdims=True) + eps)
    return (y * g.astype(jnp.float32)).astype(x.dtype)


def rope_tables(seq, dim):
    pos = jnp.arange(seq, dtype=jnp.float32)
    inv = 1.0 / (ROPE_THETA ** (jnp.arange(0, dim, 2, dtype=jnp.float32) / dim))
    ang = pos[:, None] * inv[None, :]
    ang = jnp.concatenate([ang, ang], axis=-1)
    return jnp.cos(ang), jnp.sin(ang)


def apply_rope(x, cos, sin):
    shape = (cos.shape[0],) + (1,) * (x.ndim - 3) + (cos.shape[1],)
    c, s = cos.reshape(shape), sin.reshape(shape)
    xf = x.astype(jnp.float32)
    half = x.shape[-1] // 2
    rot = jnp.concatenate([-xf[..., half:], xf[..., :half]], axis=-1)
    return (xf * c + rot * s).astype(x.dtype)


def window_attention(q, k, v, sink):
    b, s, h, d = q.shape
    hkv = k.shape[2]
    g = h // hkv
    nb = s // A_BLOCK
    qb = q.reshape(b, nb, A_BLOCK, hkv, g, d)
    pad = ((0, 0), (A_BLOCK, A_BLOCK), (0, 0), (0, 0))
    kp = jnp.pad(k, pad).reshape(b, nb + 2, A_BLOCK, hkv, d)
    vp = jnp.pad(v, pad).reshape(b, nb + 2, A_BLOCK, hkv, d)
    kb = jnp.concatenate([kp[:, :-2], kp[:, 1:-1], kp[:, 2:]], axis=2)
    vb = jnp.concatenate([vp[:, :-2], vp[:, 1:-1], vp[:, 2:]], axis=2)
    qpos = jnp.arange(s).reshape(nb, A_BLOCK)
    kpos = jnp.arange(-A_BLOCK, s + A_BLOCK).reshape(nb + 2, A_BLOCK)
    kposb = jnp.concatenate([kpos[:-2], kpos[1:-1], kpos[2:]], axis=1)
    valid = ((jnp.abs(qpos[:, :, None] - kposb[:, None, :]) <= A_WINDOW)
             & (kposb[:, None, :] >= 0) & (kposb[:, None, :] < s))
    scores = jnp.einsum('bnqkgd,bnckd->bnkgqc', qb, kb).astype(jnp.float32) * (d ** -0.5)
    scores = jnp.where(valid[None, :, None, None], scores, -1e30)
    sink_l = sink.astype(jnp.float32).reshape(hkv, g)[None, None, :, :, None, None]
    m = jnp.maximum(jnp.max(scores, axis=-1, keepdims=True), sink_l)
    p = jnp.exp(scores - m)
    denom = jnp.sum(p, axis=-1, keepdims=True) + jnp.exp(sink_l - m)
    probs = (p / denom).astype(v.dtype)
    out = jnp.einsum('bnkgqc,bnckd->bnqkgd', probs, vb)
    return out.reshape(b, s, h * d)


def neighborhood_attention(q, k, v, rpb):
    b, s, h, d = q.shape
    rows = s // GRID_W
    kr = min(NA_ROWS_MAX, rows)
    kc = NA_COLS
    qg = q.reshape(b, rows, GRID_W, h, d)
    kg = k.reshape(b, rows, GRID_W, h, d)
    vg = v.reshape(b, rows, GRID_W, h, d)
    col = jnp.arange(GRID_W)
    col_start = jnp.clip(col - kc // 2, 0, GRID_W - kc)
    col_idx = col_start[:, None] + jnp.arange(kc)[None, :]
    dc = col_idx - col[:, None]
    row_starts = jnp.clip(jnp.arange(rows) - kr // 2, 0, rows - kr)
    rpb_f = rpb.astype(jnp.float32)
    scale = d ** -0.5

    def one_row(args):
        r, rs = args
        q_r = lax.dynamic_index_in_dim(qg, r, axis=1, keepdims=False)
        k_rows = lax.dynamic_slice_in_dim(kg, rs, kr, axis=1)
        v_rows = lax.dynamic_slice_in_dim(vg, rs, kr, axis=1)
        k_nb = k_rows[:, :, col_idx]
        v_nb = v_rows[:, :, col_idx]
        sc = jnp.einsum('bwhd,biwjhd->bhwij', q_r, k_nb).astype(jnp.float32) * scale
        dr = rs + jnp.arange(kr) - r
        bias = rpb_f[:, dr[None, :, None] + NA_ROWS_MAX - 1,
                     dc[:, None, :] + NA_COLS - 1]
        sc = sc + bias[None]
        p = jax.nn.softmax(sc.reshape(b, h, GRID_W, kr * kc), axis=-1)
        p = p.reshape(b, h, GRID_W, kr, kc).astype(v.dtype)
        return jnp.einsum('bhwij,biwjhd->bwhd', p, v_nb)

    outs = lax.map(one_row, (jnp.arange(rows), row_starts))
    return outs.transpose(1, 0, 2, 3, 4).reshape(b, s, h * d)


def diff_attention(q, k, v, lam, subln_g, lambda_init):
    b, s, h, _, d = q.shape
    nb = s // C_BLOCK
    qb = q.reshape(b, nb, C_BLOCK, h, 2, d).transpose(1, 0, 2, 3, 4, 5)
    scale = d ** -0.5

    def one_block(qblk):
        sc = jnp.einsum('bqhtd,bkhtd->bhtqk', qblk, k).astype(jnp.float32) * scale
        p = jax.nn.softmax(sc, axis=-1)
        a = p[:, :, 0] - lam * p[:, :, 1]
        return jnp.einsum('bhqk,bkhe->bqhe', a.astype(v.dtype), v)

    o = lax.map(one_block, qb)
    o = o.transpose(1, 0, 2, 3, 4).reshape(b, s, h, 2 * d)
    o = (rms_norm(o, subln_g, SUBLN_EPS).astype(jnp.float32) * (1.0 - lambda_init)).astype(v.dtype)
    return o.reshape(b, s, h * 2 * d)


def even_layer(hn, w_in, sink, rpb, w_out, cos, sin):
    b, s, _ = hn.shape
    proj = jnp.einsum('bsd,de->bse', hn, w_in)
    qa, ka, va, za, qb, kb, vb, zb = jnp.split(proj, _offsets(EVEN_SPLITS), axis=-1)
    qa = apply_rope(qa.reshape(b, s, A_HEADS, HEAD_DIM), cos, sin)
    ka = apply_rope(ka.reshape(b, s, A_KV_HEADS, HEAD_DIM), cos, sin)
    va = va.reshape(b, s, A_KV_HEADS, HEAD_DIM)
    out_a = window_attention(qa, ka, va, sink) * jax.nn.silu(za)
    qb = qb.reshape(b, s, B_HEADS, HEAD_DIM)
    kb = kb.reshape(b, s, B_HEADS, HEAD_DIM)
    vb = vb.reshape(b, s, B_HEADS, HEAD_DIM)
    out_b = neighborhood_attention(qb, kb, vb, rpb) * jax.nn.silu(zb)
    mixed = jnp.concatenate([out_a, out_b], axis=-1)
    return jnp.einsum('bse,ed->bsd', mixed, w_out)


def odd_layer(hn, w_in, lq1, lk1, lq2, lk2, subln_g, w_out, cos, sin, lambda_init):
    b, s, _ = hn.shape
    proj = jnp.einsum('bsd,de->bse', hn, w_in)
    q, k, v, z = jnp.split(proj, _offsets([C_W, C_W, C_W, C_W]), axis=-1)
    q = apply_rope(q.reshape(b, s, C_HEADS, 2, C_DIM), cos, sin)
    k = apply_rope(k.reshape(b, s, C_HEADS, 2, C_DIM), cos, sin)
    v = v.reshape(b, s, C_HEADS, 2 * C_DIM)
    f32 = jnp.float32
    lam = (jnp.exp(jnp.sum(lq1.astype(f32) * lk1.astype(f32)))
           - jnp.exp(jnp.sum(lq2.astype(f32) * lk2.astype(f32))) + lambda_init)
    out = diff_attention(q, k, v, lam, subln_g, lambda_init) * jax.nn.silu(z)
    return jnp.einsum('bse,ed->bsd', out, w_out)


def setup_inputs(seed: int = 0) -> dict:
    key = jax.random.key(seed)
    ks = jax.random.split(key, 16)
    f32 = jnp.float32
    nrm = lambda k, shape, scale: jax.random.normal(k, shape, f32) * scale
    return {
        "x": nrm(ks[0], (BATCH, SEQ, D_MODEL), 1.0),
        "norm_g": 1.0 + nrm(ks[1], (DEPTH, D_MODEL), 0.02),
        "w_in_even": nrm(ks[2], (N_EVEN, D_MODEL, EVEN_IN), D_MODEL ** -0.5),
        "sink_a": nrm(ks[3], (N_EVEN, A_HEADS), 0.5),
        "rpb_b": nrm(ks[4], (N_EVEN, B_HEADS, 2 * NA_ROWS_MAX - 1, 2 * NA_COLS - 1), 0.1),
        "w_out_even": nrm(ks[5], (N_EVEN, EVEN_MIX, D_MODEL), EVEN_MIX ** -0.5),
        "w_in_odd": nrm(ks[6], (N_ODD, D_MODEL, ODD_IN), D_MODEL ** -0.5),
        "lambda_q1": nrm(ks[7], (N_ODD, C_DIM), 0.1),
        "lambda_k1": nrm(ks[8], (N_ODD, C_DIM), 0.1),
        "lambda_q2": nrm(ks[9], (N_ODD, C_DIM), 0.1),
        "lambda_k2": nrm(ks[10], (N_ODD, C_DIM), 0.1),
        "subln_g": 1.0 + nrm(ks[11], (N_ODD, 2 * C_DIM), 0.02),
        "w_out_odd": nrm(ks[12], (N_ODD, C_W, D_MODEL), C_W ** -0.5),
        "final_g": 1.0 + nrm(ks[13], (D_MODEL,), 0.02),
    }


def reference(x, norm_g, w_in_even, sink_a, rpb_b, w_out_even, w_in_odd,
              lambda_q1, lambda_k1, lambda_q2, lambda_k2, subln_g, w_out_odd, final_g):
    s = x.shape[1]
    cos, sin = rope_tables(s, HEAD_DIM)
    h = x
    for layer in range(DEPTH):
        hn = rms_norm(h, norm_g[layer])
        i = layer // 2
        if layer % 2 == 0:
            h = h + even_layer(hn, w_in_even[i], sink_a[i], rpb_b[i], w_out_even[i], cos, sin)
        else:
            lambda_init = 0.8 - 0.6 * math.exp(-0.3 * layer)
            h = h + odd_layer(hn, w_in_odd[i], lambda_q1[i], lambda_k1[i], lambda_q2[i],
                              lambda_k2[i], subln_g[i], w_out_odd[i], cos, sin, lambda_init)
    return rms_norm(h, final_g)
```

```python
import functools
import math

import jax
import jax.numpy as jnp
import numpy as np
from jax import lax
from jax.experimental import pallas as pl
from jax.experimental.pallas import tpu as pltpu

F32 = jnp.float32
BF16 = jnp.bfloat16

HEAD_DIM = 128
ROPE_THETA = 10000.0
RMS_EPS = 1e-6
SUBLN_EPS = 1e-5
A_WINDOW = 128
A_BLOCK = 128
A_GROUP = 4
GRID_W = 64
NA_ROWS_MAX = 8
NA_COLS = 16
NA_Q_ROWS = 4
NA_K_ROWS = NA_Q_ROWS + NA_ROWS_MAX
MASK_VALUE = -1e30

VMEM_LIMIT_BYTES = 48 * 1024 * 1024

_NT_DIMS = (((1,), (1,)), ((), ()))


def _silu(z):
    return z * (1.0 / (1.0 + jnp.exp(-z)))


def _params(*semantics):
    return pltpu.CompilerParams(dimension_semantics=semantics,
                                vmem_limit_bytes=VMEM_LIMIT_BYTES)


def _rmsnorm_kernel(x_ref, g_ref, o_ref, *, eps):
    x = x_ref[...]
    ms = jnp.mean(x * x, axis=-1, keepdims=True)
    o_ref[...] = (x * lax.rsqrt(ms + eps) * g_ref[...]).astype(o_ref.dtype)


def _rmsnorm(x, g, eps, out_dtype, tm=256):
    m, d = x.shape
    return pl.pallas_call(
        functools.partial(_rmsnorm_kernel, eps=eps),
        grid=(m // tm,),
        in_specs=[pl.BlockSpec((tm, d), lambda i: (i, 0)),
                  pl.BlockSpec((1, d), lambda i: (0, 0))],
        out_specs=pl.BlockSpec((tm, d), lambda i: (i, 0)),
        out_shape=jax.ShapeDtypeStruct((m, d), out_dtype),
        compiler_params=_params("arbitrary"),
        name="rmsnorm",
    )(x, g.reshape(1, d).astype(F32))


def _mm_kernel(*refs, mode):
    if mode == "rope":
        x_ref, w_ref, cos_ref, sin_ref, o_ref = refs
    elif mode == "residual":
        x_ref, w_ref, r_ref, o_ref = refs
    else:
        x_ref, w_ref, o_ref = refs
    acc = jnp.dot(x_ref[...], w_ref[...], preferred_element_type=F32)
    if mode == "rope":
        cos = cos_ref[...]
        sin = sin_ref[...]
        hd = cos.shape[1]
        for c in range(acc.shape[1] // hd):
            xc = acc[:, c * hd:(c + 1) * hd]
            rot = pltpu.roll(xc, hd // 2, axis=1)
            o_ref[:, c * hd:(c + 1) * hd] = (xc * cos + rot * sin).astype(o_ref.dtype)
    elif mode == "residual":
        o_ref[...] = (r_ref[...] + acc).astype(o_ref.dtype)
    else:
        o_ref[...] = acc.astype(o_ref.dtype)


def _matmul(x, w, n_tiles, w_tile_of, out_dtype, mode="plain", extra=(), tm=512, tn=512):
    m, k = x.shape
    in_specs = [pl.BlockSpec((tm, k), lambda i, j: (i, 0)),
                pl.BlockSpec((k, tn), lambda i, j: (0, w_tile_of(j)))]
    if mode == "rope":
        hd = extra[0].shape[1]
        in_specs += [pl.BlockSpec((tm, hd), lambda i, j: (i, 0))] * 2
    elif mode == "residual":
        in_specs += [pl.BlockSpec((tm, tn), lambda i, j: (i, j))]
    return pl.pallas_call(
        functools.partial(_mm_kernel, mode=mode),
        grid=(m // tm, n_tiles),
        in_specs=in_specs,
        out_specs=pl.BlockSpec((tm, tn), lambda i, j: (i, j)),
        out_shape=jax.ShapeDtypeStruct((m, n_tiles * tn), out_dtype),
        compiler_params=_params("arbitrary", "arbitrary"),
        name="matmul_" + mode,
    )(x, w, *extra)


def _window_kernel(sink_ref, q_ref, kp_ref, kc_ref, kn_ref, vp_ref, vc_ref, vn_ref,
                   z_ref, o_ref, *, seq, scale):
    kvh = pl.program_id(0)
    n = pl.program_id(1)
    blk, hd = kc_ref.shape
    k = jnp.concatenate([kp_ref[...], kc_ref[...], kn_ref[...]], axis=0)
    v = jnp.concatenate([vp_ref[...], vc_ref[...], vn_ref[...]], axis=0)
    qpos = n * blk + lax.broadcasted_iota(jnp.int32, (blk, 3 * blk), 0)
    kpos = (n - 1) * blk + lax.broadcasted_iota(jnp.int32, (blk, 3 * blk), 1)
    valid = (jnp.abs(qpos - kpos) <= A_WINDOW) & (kpos >= 0) & (kpos < seq)
    for gi in range(A_GROUP):
        cols = slice(gi * hd, (gi + 1) * hd)
        s = lax.dot_general(q_ref[:, cols], k, _NT_DIMS, preferred_element_type=F32) * scale
        s = jnp.where(valid, s, MASK_VALUE)
        sink = sink_ref[kvh * A_GROUP + gi]
        m = jnp.maximum(jnp.max(s, axis=-1, keepdims=True), sink)
        p = jnp.exp(s - m)
        denom = jnp.sum(p, axis=-1, keepdims=True) + jnp.exp(sink - m)
        o = jnp.dot(p.astype(BF16), v, preferred_element_type=F32) / denom
        o_ref[:, cols] = (o * _silu(z_ref[:, cols])).astype(o_ref.dtype)


def _window_attention(qk, vqkv, z, sink, kv_heads):
    seq = qk.shape[0]
    hd, blk = HEAD_DIM, A_BLOCK
    nb = seq // blk
    q_heads = kv_heads * A_GROUP
    kspec = lambda f: pl.BlockSpec((blk, hd), lambda h, n: (f(n), q_heads + h))
    vspec = lambda f: pl.BlockSpec((blk, hd), lambda h, n: (f(n), h))
    prev = lambda n: jnp.maximum(n - 1, 0)
    cur = lambda n: n
    nxt = lambda n: jnp.minimum(n + 1, nb - 1)
    wide = pl.BlockSpec((blk, A_GROUP * hd), lambda h, n: (n, h))
    return pl.pallas_call(
        functools.partial(_window_kernel, seq=seq, scale=hd ** -0.5),
        grid=(kv_heads, nb),
        in_specs=[pl.BlockSpec(memory_space=pltpu.SMEM), wide,
                  kspec(prev), kspec(cur), kspec(nxt),
                  vspec(prev), vspec(cur), vspec(nxt), wide],
        out_specs=wide,
        out_shape=jax.ShapeDtypeStruct((seq, q_heads * hd), BF16),
        compiler_params=_params("arbitrary", "arbitrary"),
        name="window_attention",
    )(sink.astype(F32), qk, qk, qk, qk, vqkv, vqkv, vqkv, z)


def _na_kernel(q_ref, k_ref, v_ref, b_ref, z_ref, o_ref, *, rows, scale):
    g = pl.program_id(1)
    key_row0 = jnp.clip(NA_Q_ROWS * g - NA_ROWS_MAX // 2, 0, rows - NA_K_ROWS)
    start = pl.multiple_of(key_row0 * GRID_W, GRID_W)
    k = k_ref[pl.ds(start, NA_K_ROWS * GRID_W), :]
    v = v_ref[pl.ds(start, NA_K_ROWS * GRID_W), :]
    s = lax.dot_general(q_ref[...], k, _NT_DIMS, preferred_element_type=F32) * scale
    s = s + b_ref[...]
    m = jnp.max(s, axis=-1, keepdims=True)
    p = jnp.exp(s - m)
    l = jnp.sum(p, axis=-1, keepdims=True)
    o = jnp.dot(p.astype(BF16), v, preferred_element_type=F32) / l
    o_ref[...] = (o * _silu(z_ref[...])).astype(o_ref.dtype)


def _na_row_tables(rows):
    kr = min(NA_ROWS_MAX, rows)
    groups = rows // NA_Q_ROWS
    variants, variant_of = [], []
    for g in range(groups):
        rq = NA_Q_ROWS * g + np.arange(NA_Q_ROWS)[:, None]
        key_row0 = int(np.clip(NA_Q_ROWS * g - NA_ROWS_MAX // 2, 0, rows - NA_K_ROWS))
        rk = key_row0 + np.arange(NA_K_ROWS)[None, :]
        rs = np.clip(rq - kr // 2, 0, rows - kr)
        valid = (rk >= rs) & (rk < rs + kr)
        assert (valid.sum(axis=1) == kr).all()
        dr = np.where(valid, rk - rq, 0)
        key = (dr.tobytes(), valid.tobytes())
        for idx, (other, _, _) in enumerate(variants):
            if other == key:
                variant_of.append(idx)
                break
        else:
            variant_of.append(len(variants))
            variants.append((key, dr, valid))
    assert variant_of == [0] + [1] * (groups - 2) + [2], variant_of
    dr = np.stack([v[1] for v in variants])
    valid = np.stack([v[2] for v in variants])
    return dr, valid


def _na_bias(rpb, rows):
    heads = rpb.shape[0]
    col = np.arange(GRID_W)
    col_start = np.clip(col - NA_COLS // 2, 0, GRID_W - NA_COLS)
    valid_c = (col[None, :] >= col_start[:, None]) & (col[None, :] < col_start[:, None] + NA_COLS)
    dc = np.where(valid_c, col[None, :] - col[:, None], 0) + NA_COLS - 1
    dr, valid_r = _na_row_tables(rows)
    by_col = rpb.astype(F32)[:, :, dc]
    by_col = jnp.where(valid_c[None, None], by_col, MASK_VALUE)
    full = by_col[:, dr + NA_ROWS_MAX - 1]
    full = jnp.where(valid_r[None, :, :, :, None, None], full, MASK_VALUE)
    full = full.transpose(0, 1, 2, 4, 3, 5)
    return full.reshape(heads, 3, NA_Q_ROWS * GRID_W, NA_K_ROWS * GRID_W)


def _neighborhood_attention(vqkv, z, rpb, heads, col0):
    seq = vqkv.shape[0]
    hd = HEAD_DIM
    rows = seq // GRID_W
    groups = rows // NA_Q_ROWS
    tq = NA_Q_ROWS * GRID_W
    tk = NA_K_ROWS * GRID_W
    bias = _na_bias(rpb, rows)
    variant = lambda g: jnp.where(g == 0, 0, jnp.where(g == groups - 1, 2, 1))
    return pl.pallas_call(
        functools.partial(_na_kernel, rows=rows, scale=hd ** -0.5),
        grid=(heads, groups),
        in_specs=[pl.BlockSpec((tq, hd), lambda h, g: (g, col0 + h)),
                  pl.BlockSpec((seq, hd), lambda h, g: (0, col0 + heads + h)),
                  pl.BlockSpec((seq, hd), lambda h, g: (0, col0 + 2 * heads + h)),
                  pl.BlockSpec((None, None, tq, tk), lambda h, g: (h, variant(g), 0, 0)),
                  pl.BlockSpec((tq, hd), lambda h, g: (g, heads + h))],
        out_specs=pl.BlockSpec((tq, hd), lambda h, g: (g, h)),
        out_shape=jax.ShapeDtypeStruct((seq, heads * hd), BF16),
        compiler_params=_params("arbitrary", "arbitrary"),
        name="neighborhood_attention",
    )(vqkv, vqkv, vqkv, bias, z)


def _diff_kernel(lq1_ref, lk1_ref, lq2_ref, lk2_ref, g_ref, q_ref, k_ref, v_ref, z_ref, o_ref,
                 *, tk, lambda_init, scale):
    tq = q_ref.shape[0]
    seq, width = v_ref.shape
    dc = width // 2
    q0 = q_ref[:, :dc]
    q1 = q_ref[:, dc:]

    def update(s, m, l, acc, vc):
        m_new = jnp.maximum(m, jnp.max(s, axis=-1, keepdims=True))
        alpha = jnp.exp((m - m_new) * scale)
        p = jnp.exp((s - m_new) * scale)
        l_new = alpha * l + jnp.sum(p, axis=-1, keepdims=True)
        acc_new = alpha * acc + jnp.dot(p.astype(BF16), vc, preferred_element_type=F32)
        return m_new, l_new, acc_new

    def body(c, carry):
        m0, l0, a0, m1, l1, a1 = carry
        off = pl.multiple_of(c * tk, tk)
        kc = k_ref[pl.ds(off, tk), :]
        vc = v_ref[pl.ds(off, tk), :]
        s0 = lax.dot_general(q0, kc[:, :dc], _NT_DIMS, preferred_element_type=F32)
        s1 = lax.dot_general(q1, kc[:, dc:], _NT_DIMS, preferred_element_type=F32)
        m0, l0, a0 = update(s0, m0, l0, a0, vc)
        m1, l1, a1 = update(s1, m1, l1, a1, vc)
        return m0, l0, a0, m1, l1, a1

    m_init = jnp.full((tq, 1), MASK_VALUE, F32)
    l_init = jnp.zeros((tq, 1), F32)
    a_init = jnp.zeros((tq, width), F32)
    _, l0, a0, _, l1, a1 = lax.fori_loop(
        0, seq // tk, body, (m_init, l_init, a_init, m_init, l_init, a_init))

    lam = (jnp.exp(jnp.sum(lq1_ref[...] * lk1_ref[...], axis=-1, keepdims=True))
           - jnp.exp(jnp.sum(lq2_ref[...] * lk2_ref[...], axis=-1, keepdims=True))
           + lambda_init)
    o = a0 / l0 - lam * (a1 / l1)
    ms = jnp.mean(o * o, axis=-1, keepdims=True)
    y = o * lax.rsqrt(ms + SUBLN_EPS) * g_ref[...]
    y = y * (1.0 - lambda_init)
    o_ref[...] = (y * _silu(z_ref[...])).astype(o_ref.dtype)


def _diff_attention(qk, v, z, lq1, lk1, lq2, lk2, subln_g, lambda_init, heads, tq=256, tk=512):
    seq = qk.shape[0]
    width = 2 * HEAD_DIM
    row = lambda a: a.reshape(1, -1).astype(F32)
    small = lambda n: pl.BlockSpec((1, n), lambda h, i: (0, 0))
    return pl.pallas_call(
        functools.partial(_diff_kernel, tk=tk, lambda_init=lambda_init, scale=HEAD_DIM ** -0.5),
        grid=(heads, seq // tq),
        in_specs=[small(HEAD_DIM)] * 4 + [small(width)] + [
            pl.BlockSpec((tq, width), lambda h, i: (i, h)),
            pl.BlockSpec((seq, width), lambda h, i: (0, heads + h)),
            pl.BlockSpec((seq, width), lambda h, i: (0, h)),
            pl.BlockSpec((tq, width), lambda h, i: (i, h))],
        out_specs=pl.BlockSpec((tq, width), lambda h, i: (i, h)),
        out_shape=jax.ShapeDtypeStruct((seq, heads * width), BF16),
        compiler_params=_params("arbitrary", "arbitrary"),
        name="diff_attention",
    )(row(lq1), row(lk1), row(lq2), row(lk2), row(subln_g), qk, qk, v, z)


def _rope_tables(seq, dim):
    pos = jnp.arange(seq, dtype=F32)
    inv = 1.0 / (ROPE_THETA ** (jnp.arange(0, dim, 2, dtype=F32) / dim))
    ang = pos[:, None] * inv[None, :]
    ang = jnp.concatenate([ang, ang], axis=-1)
    sign = jnp.where(jnp.arange(dim) < dim // 2, -1.0, 1.0).astype(F32)
    return jnp.cos(ang), jnp.sin(ang) * sign[None, :]


def _even_layer(h, g, w_in, sink, rpb, w_out, cos, sin, tn=512):
    d = h.shape[1]
    a_heads = d // (2 * HEAD_DIM)
    kv_heads = a_heads // A_GROUP
    b_heads = d // (2 * HEAD_DIM)
    a_q, a_kv, b_w = a_heads * HEAD_DIM, kv_heads * HEAD_DIM, b_heads * HEAD_DIM
    t = lambda cols: cols // tn
    o_va, o_za = t(a_q + a_kv), t(a_q + 2 * a_kv)
    o_qb = t(2 * a_q + 2 * a_kv)
    o_zb = o_qb + t(3 * b_w)
    hn = _rmsnorm(h, g, RMS_EPS, BF16)
    w = w_in.astype(BF16)
    qk = _matmul(hn, w, t(a_q + a_kv), lambda j: j, BF16, "rope", (cos, sin))
    vqkv = _matmul(hn, w, t(a_kv + 3 * b_w),
                   lambda j: jnp.where(j < t(a_kv), o_va + j, o_qb + j - t(a_kv)), BF16)
    z = _matmul(hn, w, t(a_q + b_w),
                lambda j: jnp.where(j < t(a_q), o_za + j, o_zb + j - t(a_q)), F32)
    out_a = _window_attention(qk, vqkv, z, sink, kv_heads)
    out_b = _neighborhood_attention(vqkv, z, rpb, b_heads, a_kv // HEAD_DIM)
    mixed = jnp.concatenate([out_a, out_b], axis=-1)
    return _matmul(mixed, w_out.astype(BF16), t(d), lambda j: j, F32, "residual", (h,))


def _odd_layer(h, g, w_in, lq1, lk1, lq2, lk2, subln_g, w_out, cos, sin, lambda_init, tn=512):
    d = h.shape[1]
    heads = d // (2 * HEAD_DIM)
    c_w = heads * 2 * HEAD_DIM
    t = lambda cols: cols // tn
    hn = _rmsnorm(h, g, RMS_EPS, BF16)
    w = w_in.astype(BF16)
    qk = _matmul(hn, w, t(2 * c_w), lambda j: j, BF16, "rope", (cos, sin))
    v = _matmul(hn, w, t(c_w), lambda j: t(2 * c_w) + j, BF16)
    z = _matmul(hn, w, t(c_w), lambda j: t(3 * c_w) + j, F32)
    out = _diff_attention(qk, v, z, lq1, lk1, lq2, lk2, subln_g, lambda_init, heads)
    return _matmul(out, w_out.astype(BF16), t(d), lambda j: j, F32, "residual", (h,))


def kernel(x, norm_g, w_in_even, sink_a, rpb_b, w_out_even, w_in_odd, lambda_q1, lambda_k1,
           lambda_q2, lambda_k2, subln_g, w_out_odd, final_g):
    b, s, d = x.shape
    depth = norm_g.shape[0]
    cos, sin = _rope_tables(s, HEAD_DIM)
    outs = []
    for bi in range(b):
        h = x.reshape(s, d) if b == 1 else x[bi]
        for layer in range(depth):
            i = layer // 2
            if layer % 2 == 0:
                h = _even_layer(h, norm_g[layer], w_in_even[i], sink_a[i], rpb_b[i],
                                w_out_even[i], cos, sin)
            else:
                lambda_init = 0.8 - 0.6 * math.exp(-0.3 * layer)
                h = _odd_layer(h, norm_g[layer], w_in_odd[i], lambda_q1[i], lambda_k1[i],
                               lambda_q2[i], lambda_k2[i], subln_g[i], w_out_odd[i],
                               cos, sin, lambda_init)
        outs.append(_rmsnorm(h, final_g, RMS_EPS, x.dtype))
    return outs[0].reshape(b, s, d) if b == 1 else jnp.stack(outs, axis=0)
```

```python
import functools
import math

import jax
import jax.numpy as jnp
import numpy as np
from jax import lax
from jax.experimental import pallas as pl
from jax.experimental.pallas import tpu as pltpu

F32 = jnp.float32
BF16 = jnp.bfloat16

HEAD_DIM = 128
ROPE_THETA = 10000.0
RMS_EPS = 1e-6
SUBLN_EPS = 1e-5
A_WINDOW = 128
A_BLOCK = 128
A_GROUP = 4
GRID_W = 64
NA_ROWS_MAX = 8
NA_COLS = 16
NA_Q_ROWS = 4
NA_K_ROWS = NA_Q_ROWS + NA_ROWS_MAX
MASK_VALUE = -1e30

VMEM_LIMIT_BYTES = 48 * 1024 * 1024
DIFF_VMEM_LIMIT_BYTES = 58 * 1024 * 1024

_NT_DIMS = (((1,), (1,)), ((), ()))


def _silu(z):
    return z * (1.0 / (1.0 + jnp.exp(-z)))


def _params(*semantics, vmem_limit_bytes=VMEM_LIMIT_BYTES):
    return pltpu.CompilerParams(dimension_semantics=semantics,
                                vmem_limit_bytes=vmem_limit_bytes)


def _rmsnorm_kernel(x_ref, g_ref, o_ref, *, eps):
    x = x_ref[...]
    ms = jnp.mean(x * x, axis=-1, keepdims=True)
    o_ref[...] = (x * lax.rsqrt(ms + eps) * g_ref[...]).astype(o_ref.dtype)


def _rmsnorm(x, g, eps, out_dtype, tm=256):
    m, d = x.shape
    return pl.pallas_call(
        functools.partial(_rmsnorm_kernel, eps=eps),
        grid=(m // tm,),
        in_specs=[pl.BlockSpec((tm, d), lambda i: (i, 0)),
                  pl.BlockSpec((1, d), lambda i: (0, 0))],
        out_specs=pl.BlockSpec((tm, d), lambda i: (i, 0)),
        out_shape=jax.ShapeDtypeStruct((m, d), out_dtype),
        compiler_params=_params("arbitrary"),
        name="rmsnorm",
    )(x, g.reshape(1, d).astype(F32))


def _mm_kernel(*refs, mode, scaled_tiles, factor):
    if mode == "rope":
        x_ref, w_ref, cos_ref, sin_ref, o_ref = refs
    elif mode == "residual":
        x_ref, w_ref, r_ref, o_ref = refs
    else:
        x_ref, w_ref, o_ref = refs
    acc = jnp.dot(x_ref[...], w_ref[...], preferred_element_type=F32)
    if mode == "rope":
        cos = cos_ref[...]
        sin = sin_ref[...]
        if scaled_tiles:
            f = jnp.where(pl.program_id(1) < scaled_tiles, factor, 1.0).astype(F32)
            cos = cos * f
            sin = sin * f
        hd = cos.shape[1]
        for c in range(acc.shape[1] // hd):
            xc = acc[:, c * hd:(c + 1) * hd]
            rot = pltpu.roll(xc, hd // 2, axis=1)
            o_ref[:, c * hd:(c + 1) * hd] = (xc * cos + rot * sin).astype(o_ref.dtype)
    elif mode == "residual":
        o_ref[...] = (r_ref[...] + acc).astype(o_ref.dtype)
    else:
        o_ref[...] = acc.astype(o_ref.dtype)


def _matmul(x, w, n_tiles, w_tile_of, out_dtype, mode="plain", extra=(), scaled_tiles=0,
            factor=1.0, tm=1024, tn=512):
    m, k = x.shape
    in_specs = [pl.BlockSpec((tm, k), lambda i, j: (i, 0)),
                pl.BlockSpec((k, tn), lambda i, j: (0, w_tile_of(j)))]
    if mode == "rope":
        hd = extra[0].shape[1]
        in_specs += [pl.BlockSpec((tm, hd), lambda i, j: (i, 0))] * 2
    elif mode == "residual":
        in_specs += [pl.BlockSpec((tm, tn), lambda i, j: (i, j))]
    return pl.pallas_call(
        functools.partial(_mm_kernel, mode=mode, scaled_tiles=scaled_tiles, factor=factor),
        grid=(m // tm, n_tiles),
        in_specs=in_specs,
        out_specs=pl.BlockSpec((tm, tn), lambda i, j: (i, j)),
        out_shape=jax.ShapeDtypeStruct((m, n_tiles * tn), out_dtype),
        compiler_params=_params("arbitrary", "arbitrary"),
        name="matmul_" + mode,
    )(x, w, *extra)


def _window_kernel(sink_ref, q_ref, kp_ref, kc_ref, kn_ref, vp_ref, vc_ref, vn_ref,
                   z_ref, o_ref, *, seq, scale):
    kvh = pl.program_id(0)
    n = pl.program_id(1)
    blk, hd = kc_ref.shape
    k = jnp.concatenate([kp_ref[...], kc_ref[...], kn_ref[...]], axis=0)
    v = jnp.concatenate([vp_ref[...], vc_ref[...], vn_ref[...]], axis=0)
    qpos = n * blk + lax.broadcasted_iota(jnp.int32, (blk, 3 * blk), 0)
    kpos = (n - 1) * blk + lax.broadcasted_iota(jnp.int32, (blk, 3 * blk), 1)
    valid = (jnp.abs(qpos - kpos) <= A_WINDOW) & (kpos >= 0) & (kpos < seq)
    for gi in range(A_GROUP):
        cols = slice(gi * hd, (gi + 1) * hd)
        s = lax.dot_general(q_ref[:, cols], k, _NT_DIMS, preferred_element_type=F32) * scale
        s = jnp.where(valid, s, MASK_VALUE)
        sink = sink_ref[kvh * A_GROUP + gi]
        m = jnp.maximum(jnp.max(s, axis=-1, keepdims=True), sink)
        p = jnp.exp(s - m)
        denom = jnp.sum(p, axis=-1, keepdims=True) + jnp.exp(sink - m)
        o = jnp.dot(p.astype(BF16), v, preferred_element_type=F32) / denom
        o_ref[:, cols] = (o * _silu(z_ref[:, cols])).astype(o_ref.dtype)


def _window_attention(qk, vqkv, z, sink, kv_heads):
    seq = qk.shape[0]
    hd, blk = HEAD_DIM, A_BLOCK
    nb = seq // blk
    q_heads = kv_heads * A_GROUP
    kspec = lambda f: pl.BlockSpec((blk, hd), lambda h, n: (f(n), q_heads + h))
    vspec = lambda f: pl.BlockSpec((blk, hd), lambda h, n: (f(n), h))
    prev = lambda n: jnp.maximum(n - 1, 0)
    cur = lambda n: n
    nxt = lambda n: jnp.minimum(n + 1, nb - 1)
    wide = pl.BlockSpec((blk, A_GROUP * hd), lambda h, n: (n, h))
    return pl.pallas_call(
        functools.partial(_window_kernel, seq=seq, scale=hd ** -0.5),
        grid=(kv_heads, nb),
        in_specs=[pl.BlockSpec(memory_space=pltpu.SMEM), wide,
                  kspec(prev), kspec(cur), kspec(nxt),
                  vspec(prev), vspec(cur), vspec(nxt), wide],
        out_specs=wide,
        out_shape=jax.ShapeDtypeStruct((seq, q_heads * hd), BF16),
        compiler_params=_params("arbitrary", "arbitrary"),
        name="window_attention",
    )(sink.astype(F32), qk, qk, qk, qk, vqkv, vqkv, vqkv, z)


def _na_kernel(q_ref, k_ref, v_ref, b_ref, z_ref, o_ref, *, rows, scale):
    g = pl.program_id(1)
    key_row0 = jnp.clip(NA_Q_ROWS * g - NA_ROWS_MAX // 2, 0, rows - NA_K_ROWS)
    start = pl.multiple_of(key_row0 * GRID_W, GRID_W)
    k = k_ref[pl.ds(start, NA_K_ROWS * GRID_W), :]
    v = v_ref[pl.ds(start, NA_K_ROWS * GRID_W), :]
    s = lax.dot_general(q_ref[...], k, _NT_DIMS, preferred_element_type=F32) * scale
    s = s + b_ref[...]
    m = jnp.max(s, axis=-1, keepdims=True)
    p = jnp.exp(s - m)
    l = jnp.sum(p, axis=-1, keepdims=True)
    o = jnp.dot(p.astype(BF16), v, preferred_element_type=F32) / l
    o_ref[...] = (o * _silu(z_ref[...])).astype(o_ref.dtype)


def _na_row_tables(rows):
    kr = min(NA_ROWS_MAX, rows)
    groups = rows // NA_Q_ROWS
    variants, variant_of = [], []
    for g in range(groups):
        rq = NA_Q_ROWS * g + np.arange(NA_Q_ROWS)[:, None]
        key_row0 = int(np.clip(NA_Q_ROWS * g - NA_ROWS_MAX // 2, 0, rows - NA_K_ROWS))
        rk = key_row0 + np.arange(NA_K_ROWS)[None, :]
        rs = np.clip(rq - kr // 2, 0, rows - kr)
        valid = (rk >= rs) & (rk < rs + kr)
        assert (valid.sum(axis=1) == kr).all()
        dr = np.where(valid, rk - rq, 0)
        key = (dr.tobytes(), valid.tobytes())
        for idx, (other, _, _) in enumerate(variants):
            if other == key:
                variant_of.append(idx)
                break
        else:
            variant_of.append(len(variants))
            variants.append((key, dr, valid))
    assert variant_of == [0] + [1] * (groups - 2) + [2], variant_of
    dr = np.stack([v[1] for v in variants])
    valid = np.stack([v[2] for v in variants])
    return dr, valid


def _na_bias(rpb, rows):
    heads = rpb.shape[0]
    col = np.arange(GRID_W)
    col_start = np.clip(col - NA_COLS // 2, 0, GRID_W - NA_COLS)
    valid_c = (col[None, :] >= col_start[:, None]) & (col[None, :] < col_start[:, None] + NA_COLS)
    dc = np.where(valid_c, col[None, :] - col[:, None], 0) + NA_COLS - 1
    dr, valid_r = _na_row_tables(rows)
    by_col = rpb.astype(F32)[:, :, dc]
    by_col = jnp.where(valid_c[None, None], by_col, MASK_VALUE)
    full = by_col[:, dr + NA_ROWS_MAX - 1]
    full = jnp.where(valid_r[None, :, :, :, None, None], full, MASK_VALUE)
    full = full.transpose(0, 1, 2, 4, 3, 5)
    return full.reshape(heads, 3, NA_Q_ROWS * GRID_W, NA_K_ROWS * GRID_W)


def _neighborhood_attention(vqkv, z, rpb, heads, col0):
    seq = vqkv.shape[0]
    hd = HEAD_DIM
    rows = seq // GRID_W
    groups = rows // NA_Q_ROWS
    tq = NA_Q_ROWS * GRID_W
    tk = NA_K_ROWS * GRID_W
    bias = _na_bias(rpb, rows)
    variant = lambda g: jnp.where(g == 0, 0, jnp.where(g == groups - 1, 2, 1))
    return pl.pallas_call(
        functools.partial(_na_kernel, rows=rows, scale=hd ** -0.5),
        grid=(heads, groups),
        in_specs=[pl.BlockSpec((tq, hd), lambda h, g: (g, col0 + h)),
                  pl.BlockSpec((seq, hd), lambda h, g: (0, col0 + heads + h)),
                  pl.BlockSpec((seq, hd), lambda h, g: (0, col0 + 2 * heads + h)),
                  pl.BlockSpec((None, None, tq, tk), lambda h, g: (h, variant(g), 0, 0)),
                  pl.BlockSpec((tq, hd), lambda h, g: (g, heads + h))],
        out_specs=pl.BlockSpec((tq, hd), lambda h, g: (g, h)),
        out_shape=jax.ShapeDtypeStruct((seq, heads * hd), BF16),
        compiler_params=_params("arbitrary", "arbitrary"),
        name="neighborhood_attention",
    )(vqkv, vqkv, vqkv, bias, z)


def _diff_kernel(lq1_ref, lk1_ref, lq2_ref, lk2_ref, g_ref, q_ref, k_ref, v_ref, z_ref, o_ref,
                 s_scr, p_scr, acc_scr, l_scr, m_scr, alpha_scr, *, tk, lambda_init):
    tq = q_ref.shape[0]
    seq, width = v_ref.shape
    dc = width // 2
    lanes = l_scr.shape[-1]
    n_chunks = seq // tk
    q_parts = (q_ref[:, :dc], q_ref[:, dc:])

    def chunk(ref, c):
        return ref[c * tk:(c + 1) * tk, :]

    def scores(c, slot):
        kc = chunk(k_ref, c)
        for t in range(2):
            s_scr[slot, t] = lax.dot_general(q_parts[t], kc[:, t * dc:(t + 1) * dc], _NT_DIMS,
                                             preferred_element_type=F32)

    def softmax_chunk(slot, first):
        for t in range(2):
            blocks = [s_scr[slot, t, :, j * lanes:(j + 1) * lanes] for j in range(tk // lanes)]
            m_new = jnp.max(functools.reduce(jnp.maximum, blocks), axis=-1, keepdims=True)
            if first:
                m_new = jnp.broadcast_to(m_new, (tq, lanes))
            else:
                m_old = m_scr[t]
                m_new = jnp.maximum(m_old, m_new)
            ps = [jnp.exp2(blk - m_new) for blk in blocks]
            l_new = functools.reduce(jnp.add, ps)
            if not first:
                alpha = jnp.exp2(m_old - m_new)
                alpha_scr[slot, t] = alpha
                l_new = alpha * l_scr[t] + l_new
            l_scr[t] = l_new
            m_scr[t] = m_new
            for j, p in enumerate(ps):
                p_scr[slot, t, :, j * lanes:(j + 1) * lanes] = p.astype(BF16)

    def accumulate(slot, c, first):
        vc = chunk(v_ref, c)
        out = []
        for t in range(2):
            pv = jnp.dot(p_scr[slot, t], vc, preferred_element_type=F32)
            if not first:
                pv = jnp.concatenate(
                    [alpha_scr[slot, t] * acc_scr[t, :, j * lanes:(j + 1) * lanes]
                     + pv[:, j * lanes:(j + 1) * lanes] for j in range(width // lanes)], axis=1)
            out.append(pv)
        return out

    scores(0, 0)
    for c in range(n_chunks):
        softmax_chunk(c % 2, first=(c == 0))
        if c >= 1:
            for t, acc in enumerate(accumulate((c - 1) % 2, c - 1, first=(c == 1))):
                acc_scr[t] = acc
        if c + 1 < n_chunks:
            scores(c + 1, (c + 1) % 2)

    last = n_chunks - 1
    outs = [acc / jnp.sum(l_scr[t], axis=-1, keepdims=True)
            for t, acc in enumerate(accumulate(last % 2, last, first=(last == 0)))]

    lam = (jnp.exp(jnp.sum(lq1_ref[...] * lk1_ref[...], axis=-1, keepdims=True))
           - jnp.exp(jnp.sum(lq2_ref[...] * lk2_ref[...], axis=-1, keepdims=True))
           + lambda_init)
    o = outs[0] - lam * outs[1]
    ms = jnp.mean(o * o, axis=-1, keepdims=True)
    y = o * lax.rsqrt(ms + SUBLN_EPS) * g_ref[...]
    y = y * (1.0 - lambda_init)
    o_ref[...] = (y * _silu(z_ref[...])).astype(o_ref.dtype)


def _diff_attention(qk, v, z, lq1, lk1, lq2, lk2, subln_g, lambda_init, heads, tq=512, tk=1024):
    seq = qk.shape[0]
    width = 2 * HEAD_DIM
    row = lambda a: a.reshape(1, -1).astype(F32)
    small = lambda n: pl.BlockSpec((1, n), lambda h, i: (0, 0))
    return pl.pallas_call(
        functools.partial(_diff_kernel, tk=tk, lambda_init=lambda_init),
        grid=(heads, seq // tq),
        in_specs=[small(HEAD_DIM)] * 4 + [small(width)] + [
            pl.BlockSpec((tq, width), lambda h, i: (i, h)),
            pl.BlockSpec((seq, width), lambda h, i: (0, heads + h)),
            pl.BlockSpec((seq, width), lambda h, i: (0, h)),
            pl.BlockSpec((tq, width), lambda h, i: (i, h))],
        out_specs=pl.BlockSpec((tq, width), lambda h, i: (i, h)),
        out_shape=jax.ShapeDtypeStruct((seq, heads * width), BF16),
        scratch_shapes=[pltpu.VMEM((2, 2, tq, tk), F32),
                        pltpu.VMEM((2, 2, tq, tk), BF16),
                        pltpu.VMEM((2, tq, width), F32),
                        pltpu.VMEM((2, tq, HEAD_DIM), F32),
                        pltpu.VMEM((2, tq, HEAD_DIM), F32),
                        pltpu.VMEM((2, 2, tq, HEAD_DIM), F32)],
        compiler_params=_params("arbitrary", "arbitrary", vmem_limit_bytes=DIFF_VMEM_LIMIT_BYTES),
        name="diff_attention",
    )(row(lq1), row(lk1), row(lq2), row(lk2), row(subln_g), qk, qk, v, z)


def _rope_tables(seq, dim):
    pos = jnp.arange(seq, dtype=F32)
    inv = 1.0 / (ROPE_THETA ** (jnp.arange(0, dim, 2, dtype=F32) / dim))
    ang = pos[:, None] * inv[None, :]
    ang = jnp.concatenate([ang, ang], axis=-1)
    sign = jnp.where(jnp.arange(dim) < dim // 2, -1.0, 1.0).astype(F32)
    return jnp.cos(ang), jnp.sin(ang) * sign[None, :]


def _even_layer(h, g, w_in, sink, rpb, w_out, cos, sin, tn=512):
    d = h.shape[1]
    a_heads = d // (2 * HEAD_DIM)
    kv_heads = a_heads // A_GROUP
    b_heads = d // (2 * HEAD_DIM)
    a_q, a_kv, b_w = a_heads * HEAD_DIM, kv_heads * HEAD_DIM, b_heads * HEAD_DIM
    t = lambda cols: cols // tn
    o_va, o_za = t(a_q + a_kv), t(a_q + 2 * a_kv)
    o_qb = t(2 * a_q + 2 * a_kv)
    o_zb = o_qb + t(3 * b_w)
    hn = _rmsnorm(h, g, RMS_EPS, BF16)
    w = w_in.astype(BF16)
    qk = _matmul(hn, w, t(a_q + a_kv), lambda j: j, BF16, "rope", (cos, sin))
    vqkv = _matmul(hn, w, t(a_kv + 3 * b_w),
                   lambda j: jnp.where(j < t(a_kv), o_va + j, o_qb + j - t(a_kv)), BF16)
    z = _matmul(hn, w, t(a_q + b_w),
                lambda j: jnp.where(j < t(a_q), o_za + j, o_zb + j - t(a_q)), F32)
    out_a = _window_attention(qk, vqkv, z, sink, kv_heads)
    out_b = _neighborhood_attention(vqkv, z, rpb, b_heads, a_kv // HEAD_DIM)
    mixed = jnp.concatenate([out_a, out_b], axis=-1)
    return _matmul(mixed, w_out.astype(BF16), t(d), lambda j: j, F32, "residual", (h,))


def _odd_layer(h, g, w_in, lq1, lk1, lq2, lk2, subln_g, w_out, cos, sin, lambda_init, tn=512):
    d = h.shape[1]
    heads = d // (2 * HEAD_DIM)
    c_w = heads * 2 * HEAD_DIM
    t = lambda cols: cols // tn
    hn = _rmsnorm(h, g, RMS_EPS, BF16)
    w = w_in.astype(BF16)
    qk = _matmul(hn, w, t(2 * c_w), lambda j: j, BF16, "rope", (cos, sin),
                 scaled_tiles=t(c_w), factor=HEAD_DIM ** -0.5 * math.log2(math.e))
    v = _matmul(hn, w, t(c_w), lambda j: t(2 * c_w) + j, BF16)
    z = _matmul(hn, w, t(c_w), lambda j: t(3 * c_w) + j, F32)
    out = _diff_attention(qk, v, z, lq1, lk1, lq2, lk2, subln_g, lambda_init, heads)
    return _matmul(out, w_out.astype(BF16), t(d), lambda j: j, F32, "residual", (h,))


def kernel(x, norm_g, w_in_even, sink_a, rpb_b, w_out_even, w_in_odd, lambda_q1, lambda_k1,
           lambda_q2, lambda_k2, subln_g, w_out_odd, final_g):
    b, s, d = x.shape
    depth = norm_g.shape[0]
    cos, sin = _rope_tables(s, HEAD_DIM)
    outs = []
    for bi in range(b):
        h = x.reshape(s, d) if b == 1 else x[bi]
        for layer in range(depth):
            i = layer // 2
            if layer % 2 == 0:
                h = _even_layer(h, norm_g[layer], w_in_even[i], sink_a[i], rpb_b[i],
                                w_out_even[i], cos, sin)
            else:
                lambda_init = 0.8 - 0.6 * math.exp(-0.3 * layer)
                h = _odd_layer(h, norm_g[layer], w_in_odd[i], lambda_q1[i], lambda_k1[i],
                               lambda_q2[i], lambda_k2[i], subln_g[i], w_out_odd[i],
                               cos, sin, lambda_init)
        outs.append(_rmsnorm(h, final_g, RMS_EPS, x.dtype))
    return outs[0].reshape(b, s, d) if b == 1 else jnp.stack(outs, axis=0)
```

```python
import functools
import math

import jax
import jax.numpy as jnp
import numpy as np
from jax import lax
from jax.experimental import pallas as pl
from jax.experimental.pallas import tpu as pltpu

F32 = jnp.float32
BF16 = jnp.bfloat16

HEAD_DIM = 128
ROPE_THETA = 10000.0
RMS_EPS = 1e-6
SUBLN_EPS = 1e-5
A_WINDOW = 128
A_BLOCK = 128
A_GROUP = 4
GRID_W = 64
NA_ROWS_MAX = 8
NA_COLS = 16
NA_Q_ROWS = 4
NA_K_ROWS = NA_Q_ROWS + NA_ROWS_MAX
NA_HEADS_PER_STEP = 4
MASK_VALUE = -1e30
LOG2E = math.log2(math.e)
SOFTMAX_FACTOR = HEAD_DIM ** -0.5 * LOG2E

VMEM_LIMIT_BYTES = 48 * 1024 * 1024
DIFF_VMEM_LIMIT_BYTES = 58 * 1024 * 1024

_NT_DIMS = (((1,), (1,)), ((), ()))


def _silu(z):
    return z * (1.0 / (1.0 + jnp.exp(-z)))


def _params(*semantics, vmem_limit_bytes=VMEM_LIMIT_BYTES):
    return pltpu.CompilerParams(dimension_semantics=semantics,
                                vmem_limit_bytes=vmem_limit_bytes)


def _rmsnorm_kernel(x_ref, g_ref, o_ref, *, eps):
    x = x_ref[...]
    ms = jnp.mean(x * x, axis=-1, keepdims=True)
    o_ref[...] = (x * lax.rsqrt(ms + eps) * g_ref[...]).astype(o_ref.dtype)


def _rmsnorm(x, g, eps, out_dtype, tm=256):
    m, d = x.shape
    return pl.pallas_call(
        functools.partial(_rmsnorm_kernel, eps=eps),
        grid=(m // tm,),
        in_specs=[pl.BlockSpec((tm, d), lambda i: (i, 0)),
                  pl.BlockSpec((1, d), lambda i: (0, 0))],
        out_specs=pl.BlockSpec((tm, d), lambda i: (i, 0)),
        out_shape=jax.ShapeDtypeStruct((m, d), out_dtype),
        compiler_params=_params("arbitrary"),
        name="rmsnorm",
    )(x, g.reshape(1, d).astype(F32))


def _tile_factor(scaled, factor):
    j = pl.program_id(1)
    return jnp.where((j >= scaled[0]) & (j < scaled[1]), factor, 1.0).astype(F32)


def _mm_kernel(*refs, mode, scaled, factor):
    if mode == "rope":
        x_ref, w_ref, cos_ref, sin_ref, o_ref = refs
    elif mode == "residual":
        x_ref, w_ref, r_ref, o_ref = refs
    else:
        x_ref, w_ref, o_ref = refs
    acc = jnp.dot(x_ref[...], w_ref[...], preferred_element_type=F32)
    if mode == "rope":
        cos = cos_ref[...]
        sin = sin_ref[...]
        if scaled:
            cos = cos * _tile_factor(scaled, factor)
            sin = sin * _tile_factor(scaled, factor)
        hd = cos.shape[1]
        for c in range(acc.shape[1] // hd):
            xc = acc[:, c * hd:(c + 1) * hd]
            rot = pltpu.roll(xc, hd // 2, axis=1)
            o_ref[:, c * hd:(c + 1) * hd] = (xc * cos + rot * sin).astype(o_ref.dtype)
    elif mode == "residual":
        o_ref[...] = (r_ref[...] + acc).astype(o_ref.dtype)
    else:
        if scaled:
            acc = acc * _tile_factor(scaled, factor)
        o_ref[...] = acc.astype(o_ref.dtype)


def _matmul(x, w, n_tiles, w_tile_of, out_dtype, mode="plain", extra=(), scaled=None,
            factor=1.0, tm=1024, tn=512):
    m, k = x.shape
    in_specs = [pl.BlockSpec((tm, k), lambda i, j: (i, 0)),
                pl.BlockSpec((k, tn), lambda i, j: (0, w_tile_of(j)))]
    if mode == "rope":
        hd = extra[0].shape[1]
        in_specs += [pl.BlockSpec((tm, hd), lambda i, j: (i, 0))] * 2
    elif mode == "residual":
        in_specs += [pl.BlockSpec((tm, tn), lambda i, j: (i, j))]
    return pl.pallas_call(
        functools.partial(_mm_kernel, mode=mode, scaled=scaled, factor=factor),
        grid=(m // tm, n_tiles),
        in_specs=in_specs,
        out_specs=pl.BlockSpec((tm, tn), lambda i, j: (i, j)),
        out_shape=jax.ShapeDtypeStruct((m, n_tiles * tn), out_dtype),
        compiler_params=_params("arbitrary", "arbitrary"),
        name="matmul_" + mode,
    )(x, w, *extra)


def _window_kernel(sink_ref, mask_ref, q_ref, kp_ref, kc_ref, kn_ref, vp_ref, vc_ref, vn_ref,
                   z_ref, o_ref):
    hd = HEAD_DIM
    blk = q_ref.shape[0]
    kv_heads = kc_ref.shape[1] // hd
    mask = jnp.concatenate([mask_ref[...]] * A_GROUP, axis=0)

    def head_cols(head):
        return slice(head * hd, (head + 1) * hd)

    def window(refs, kvh):
        return jnp.concatenate([r[:, head_cols(kvh)] for r in refs], axis=0)

    scores = []
    for kvh in range(kv_heads):
        q = jnp.concatenate([q_ref[:, head_cols(kvh * A_GROUP + gi)] for gi in range(A_GROUP)],
                            axis=0)
        k = window((kp_ref, kc_ref, kn_ref), kvh)
        scores.append(lax.dot_general(q, k, _NT_DIMS, preferred_element_type=F32) + mask)
    probs, denoms = [], []
    for kvh, s in enumerate(scores):
        ps, ds = [], []
        for gi in range(A_GROUP):
            sg = s[gi * blk:(gi + 1) * blk]
            sink = sink_ref[kvh * A_GROUP + gi] * LOG2E
            m = jnp.maximum(jnp.max(sg, axis=-1, keepdims=True), sink)
            p = jnp.exp2(sg - m)
            ds.append(jnp.sum(p, axis=-1, keepdims=True) + jnp.exp2(sink - m))
            ps.append(p.astype(BF16))
        probs.append(jnp.concatenate(ps, axis=0))
        denoms.append(ds)
    for kvh, (p, ds) in enumerate(zip(probs, denoms)):
        v = window((vp_ref, vc_ref, vn_ref), kvh)
        o = jnp.dot(p, v, preferred_element_type=F32)
        for gi in range(A_GROUP):
            cols = head_cols(kvh * A_GROUP + gi)
            og = o[gi * blk:(gi + 1) * blk] / ds[gi]
            o_ref[:, cols] = (og * _silu(z_ref[:, cols])).astype(o_ref.dtype)


def _window_masks(seq):
    blk = A_BLOCK
    qpos = np.arange(blk)[:, None]
    kpos = np.arange(-blk, 2 * blk)[None, :]
    band = np.abs(qpos - kpos) <= A_WINDOW
    has_prev = np.broadcast_to(kpos >= 0, band.shape)
    has_next = np.broadcast_to(kpos < blk, band.shape)
    valid = np.stack([band & has_prev, band, band & has_next])
    if seq == blk:
        valid = valid & has_prev & has_next
    return np.where(valid, 0.0, MASK_VALUE).astype(np.float32)


def _window_attention(qk, vqkv, z, sink, kv_heads):
    seq = qk.shape[0]
    hd, blk = HEAD_DIM, A_BLOCK
    nb = seq // blk
    q_heads = kv_heads * A_GROUP
    kv_w, q_w = kv_heads * hd, q_heads * hd
    kspec = lambda f: pl.BlockSpec((blk, kv_w), lambda n: (f(n), q_w // kv_w))
    vspec = lambda f: pl.BlockSpec((blk, kv_w), lambda n: (f(n), 0))
    prev = lambda n: jnp.maximum(n - 1, 0)
    cur = lambda n: n
    nxt = lambda n: jnp.minimum(n + 1, nb - 1)
    wide = pl.BlockSpec((blk, q_w), lambda n: (n, 0))
    variant = lambda n: jnp.where(n == 0, 0, jnp.where(n == nb - 1, 2, 1))
    return pl.pallas_call(
        _window_kernel,
        grid=(nb,),
        in_specs=[pl.BlockSpec(memory_space=pltpu.SMEM),
                  pl.BlockSpec((None, blk, 3 * blk), lambda n: (variant(n), 0, 0)), wide,
                  kspec(prev), kspec(cur), kspec(nxt),
                  vspec(prev), vspec(cur), vspec(nxt), wide],
        out_specs=wide,
        out_shape=jax.ShapeDtypeStruct((seq, q_w), BF16),
        compiler_params=_params("arbitrary"),
        name="window_attention",
    )(sink.astype(F32), _window_masks(seq), qk, qk, qk, qk, vqkv, vqkv, vqkv, z)


def _na_kernel(q_ref, k0_ref, k1_ref, k2_ref, v0_ref, v1_ref, v2_ref, b_ref, z_ref, o_ref):
    hd = HEAD_DIM
    n_heads = q_ref.shape[1] // hd

    def head_cols(hh):
        return slice(hh * hd, (hh + 1) * hd)

    def window(refs, hh):
        return jnp.concatenate([r[:, head_cols(hh)] for r in refs], axis=0)

    scores = [lax.dot_general(q_ref[:, head_cols(hh)], window((k0_ref, k1_ref, k2_ref), hh),
                              _NT_DIMS, preferred_element_type=F32) + b_ref[hh]
              for hh in range(n_heads)]
    probs, denoms = [], []
    for s in scores:
        p = jnp.exp2(s - jnp.max(s, axis=-1, keepdims=True))
        denoms.append(jnp.sum(p, axis=-1, keepdims=True))
        probs.append(p.astype(BF16))
    for hh, (p, denom) in enumerate(zip(probs, denoms)):
        o = jnp.dot(p, window((v0_ref, v1_ref, v2_ref), hh), preferred_element_type=F32) / denom
        cols = head_cols(hh)
        o_ref[:, cols] = (o * _silu(z_ref[:, cols])).astype(o_ref.dtype)


def _na_row_tables(rows):
    kr = min(NA_ROWS_MAX, rows)
    groups = rows // NA_Q_ROWS
    variants, variant_of = [], []
    for g in range(groups):
        rq = NA_Q_ROWS * g + np.arange(NA_Q_ROWS)[:, None]
        key_row0 = int(np.clip(NA_Q_ROWS * g - NA_ROWS_MAX // 2, 0, rows - NA_K_ROWS))
        rk = key_row0 + np.arange(NA_K_ROWS)[None, :]
        rs = np.clip(rq - kr // 2, 0, rows - kr)
        valid = (rk >= rs) & (rk < rs + kr)
        assert (valid.sum(axis=1) == kr).all()
        dr = np.where(valid, rk - rq, 0)
        key = (dr.tobytes(), valid.tobytes())
        for idx, (other, _, _) in enumerate(variants):
            if other == key:
                variant_of.append(idx)
                break
        else:
            variant_of.append(len(variants))
            variants.append((key, dr, valid))
    assert variant_of == [0] + [1] * (groups - 2) + [2], variant_of
    dr = np.stack([v[1] for v in variants])
    valid = np.stack([v[2] for v in variants])
    return dr, valid


def _na_bias(rpb, rows):
    heads = rpb.shape[0]
    col = np.arange(GRID_W)
    col_start = np.clip(col - NA_COLS // 2, 0, GRID_W - NA_COLS)
    valid_c = (col[None, :] >= col_start[:, None]) & (col[None, :] < col_start[:, None] + NA_COLS)
    dc = np.where(valid_c, col[None, :] - col[:, None], 0) + NA_COLS - 1
    dr, valid_r = _na_row_tables(rows)
    by_col = rpb.astype(F32)[:, :, dc]
    by_col = jnp.where(valid_c[None, None], by_col, MASK_VALUE)
    full = by_col[:, dr + NA_ROWS_MAX - 1]
    full = jnp.where(valid_r[None, :, :, :, None, None], full, MASK_VALUE)
    full = full.transpose(0, 1, 2, 4, 3, 5)
    return full.reshape(heads, 3, NA_Q_ROWS * GRID_W, NA_K_ROWS * GRID_W)


def _neighborhood_attention(vqkv, z, rpb, heads, col0):
    seq = vqkv.shape[0]
    hd = HEAD_DIM
    rows = seq // GRID_W
    groups = rows // NA_Q_ROWS
    tq = NA_Q_ROWS * GRID_W
    tk = NA_K_ROWS * GRID_W
    bias = _na_bias(rpb, rows) * LOG2E
    variant = lambda g: jnp.where(g == 0, 0, jnp.where(g == groups - 1, 2, 1))
    hps = NA_HEADS_PER_STEP
    wd = hps * hd
    c0, steps = col0 // hps, heads // hps
    assert NA_K_ROWS == 3 * NA_Q_ROWS and NA_ROWS_MAX // 2 == NA_Q_ROWS
    first_block = lambda g: jnp.clip(g - 1, 0, groups - 3)
    kv_specs = [pl.BlockSpec((tq, wd), lambda h, g, i=i, c=c: (first_block(g) + i, c + h))
                for c in (c0 + steps, c0 + 2 * steps) for i in range(3)]
    return pl.pallas_call(
        _na_kernel,
        grid=(steps, groups),
        in_specs=[pl.BlockSpec((tq, wd), lambda h, g: (g, c0 + h))] + kv_specs + [
                  pl.BlockSpec((hps, None, tq, tk), lambda h, g: (h, variant(g), 0, 0)),
                  pl.BlockSpec((tq, wd), lambda h, g: (g, steps + h))],
        out_specs=pl.BlockSpec((tq, wd), lambda h, g: (g, h)),
        out_shape=jax.ShapeDtypeStruct((seq, heads * hd), BF16),
        compiler_params=_params("arbitrary", "arbitrary"),
        name="neighborhood_attention",
    )(*([vqkv] * 7), bias, z)


def _diff_kernel(lq1_ref, lk1_ref, lq2_ref, lk2_ref, g_ref, q_ref, k_ref, v_ref, z_ref, o_ref,
                 s_scr, p_scr, acc_scr, l_scr, m_scr, alpha_scr, *, tk, lambda_init):
    tq = q_ref.shape[0]
    seq, width = v_ref.shape
    dc = width // 2
    lanes = l_scr.shape[-1]
    n_chunks = seq // tk
    q_parts = (q_ref[:, :dc], q_ref[:, dc:])

    def chunk(ref, c):
        return ref[c * tk:(c + 1) * tk, :]

    def scores(c, slot):
        kc = chunk(k_ref, c)
        for t in range(2):
            s_scr[slot, t] = lax.dot_general(q_parts[t], kc[:, t * dc:(t + 1) * dc], _NT_DIMS,
                                             preferred_element_type=F32)

    def softmax_chunk(slot, first):
        for t in range(2):
            blocks = [s_scr[slot, t, :, j * lanes:(j + 1) * lanes] for j in range(tk // lanes)]
            m_new = jnp.max(functools.reduce(jnp.maximum, blocks), axis=-1, keepdims=True)
            if first:
                m_new = jnp.broadcast_to(m_new, (tq, lanes))
            else:
                m_old = m_scr[t]
                m_new = jnp.maximum(m_old, m_new)
            ps = [jnp.exp2(blk - m_new) for blk in blocks]
            l_new = functools.reduce(jnp.add, ps)
            if not first:
                alpha = jnp.exp2(m_old - m_new)
                alpha_scr[slot, t] = alpha
                l_new = alpha * l_scr[t] + l_new
            l_scr[t] = l_new
            m_scr[t] = m_new
            for j, p in enumerate(ps):
                p_scr[slot, t, :, j * lanes:(j + 1) * lanes] = p.astype(BF16)

    def accumulate(slot, c, first):
        vc = chunk(v_ref, c)
        out = []
        for t in range(2):
            pv = jnp.dot(p_scr[slot, t], vc, preferred_element_type=F32)
            if not first:
                pv = jnp.concatenate(
                    [alpha_scr[slot, t] * acc_scr[t, :, j * lanes:(j + 1) * lanes]
                     + pv[:, j * lanes:(j + 1) * lanes] for j in range(width // lanes)], axis=1)
            out.append(pv)
        return out

    scores(0, 0)
    for c in range(n_chunks):
        if c >= 1:
            for t, acc in enumerate(accumulate((c - 1) % 2, c - 1, first=(c == 1))):
                acc_scr[t] = acc
        if c + 1 < n_chunks:
            scores(c + 1, (c + 1) % 2)
        softmax_chunk(c % 2, first=(c == 0))

    last = n_chunks - 1
    outs = [acc / jnp.sum(l_scr[t], axis=-1, keepdims=True)
            for t, acc in enumerate(accumulate(last % 2, last, first=(last == 0)))]

    lam = (jnp.exp(jnp.sum(lq1_ref[...] * lk1_ref[...], axis=-1, keepdims=True))
           - jnp.exp(jnp.sum(lq2_ref[...] * lk2_ref[...], axis=-1, keepdims=True))
           + lambda_init)
    o = outs[0] - lam * outs[1]
    ms = jnp.mean(o * o, axis=-1, keepdims=True)
    y = o * lax.rsqrt(ms + SUBLN_EPS) * g_ref[...]
    y = y * (1.0 - lambda_init)
    o_ref[...] = (y * _silu(z_ref[...])).astype(o_ref.dtype)


def _diff_attention(qk, v, z, lq1, lk1, lq2, lk2, subln_g, lambda_init, heads, tq=512, tk=1024):
    seq = qk.shape[0]
    width = 2 * HEAD_DIM
    row = lambda a: a.reshape(1, -1).astype(F32)
    small = lambda n: pl.BlockSpec((1, n), lambda h, i: (0, 0))
    return pl.pallas_call(
        functools.partial(_diff_kernel, tk=tk, lambda_init=lambda_init),
        grid=(heads, seq // tq),
        in_specs=[small(HEAD_DIM)] * 4 + [small(width)] + [
            pl.BlockSpec((tq, width), lambda h, i: (i, h)),
            pl.BlockSpec((seq, width), lambda h, i: (0, heads + h)),
            pl.BlockSpec((seq, width), lambda h, i: (0, h)),
            pl.BlockSpec((tq, width), lambda h, i: (i, h))],
        out_specs=pl.BlockSpec((tq, width), lambda h, i: (i, h)),
        out_shape=jax.ShapeDtypeStruct((seq, heads * width), BF16),
        scratch_shapes=[pltpu.VMEM((2, 2, tq, tk), F32),
                        pltpu.VMEM((2, 2, tq, tk), BF16),
                        pltpu.VMEM((2, tq, width), F32),
                        pltpu.VMEM((2, tq, HEAD_DIM), F32),
                        pltpu.VMEM((2, tq, HEAD_DIM), F32),
                        pltpu.VMEM((2, 2, tq, HEAD_DIM), F32)],
        compiler_params=_params("arbitrary", "arbitrary", vmem_limit_bytes=DIFF_VMEM_LIMIT_BYTES),
        name="diff_attention",
    )(row(lq1), row(lk1), row(lq2), row(lk2), row(subln_g), qk, qk, v, z)


def _rope_tables(seq, dim):
    pos = jnp.arange(seq, dtype=F32)
    inv = 1.0 / (ROPE_THETA ** (jnp.arange(0, dim, 2, dtype=F32) / dim))
    ang = pos[:, None] * inv[None, :]
    ang = jnp.concatenate([ang, ang], axis=-1)
    sign = jnp.where(jnp.arange(dim) < dim // 2, -1.0, 1.0).astype(F32)
    return jnp.cos(ang), jnp.sin(ang) * sign[None, :]


def _even_layer(h, g, w_in, sink, rpb, w_out, cos, sin, tn=512):
    d = h.shape[1]
    a_heads = d // (2 * HEAD_DIM)
    kv_heads = a_heads // A_GROUP
    b_heads = d // (2 * HEAD_DIM)
    a_q, a_kv, b_w = a_heads * HEAD_DIM, kv_heads * HEAD_DIM, b_heads * HEAD_DIM
    t = lambda cols: cols // tn
    o_va, o_za = t(a_q + a_kv), t(a_q + 2 * a_kv)
    o_qb = t(2 * a_q + 2 * a_kv)
    o_zb = o_qb + t(3 * b_w)
    hn = _rmsnorm(h, g, RMS_EPS, BF16)
    w = w_in.astype(BF16)
    qk = _matmul(hn, w, t(a_q + a_kv), lambda j: j, BF16, "rope", (cos, sin),
                 scaled=(0, t(a_q)), factor=SOFTMAX_FACTOR)
    vqkv = _matmul(hn, w, t(a_kv + 3 * b_w),
                   lambda j: jnp.where(j < t(a_kv), o_va + j, o_qb + j - t(a_kv)), BF16,
                   scaled=(t(a_kv), t(a_kv + b_w)), factor=SOFTMAX_FACTOR)
    z = _matmul(hn, w, t(a_q + b_w),
                lambda j: jnp.where(j < t(a_q), o_za + j, o_zb + j - t(a_q)), F32)
    out_a = _window_attention(qk, vqkv, z, sink, kv_heads)
    out_b = _neighborhood_attention(vqkv, z, rpb, b_heads, a_kv // HEAD_DIM)
    mixed = jnp.concatenate([out_a, out_b], axis=-1)
    return _matmul(mixed, w_out.astype(BF16), t(d), lambda j: j, F32, "residual", (h,))


def _odd_layer(h, g, w_in, lq1, lk1, lq2, lk2, subln_g, w_out, cos, sin, lambda_init, tn=512):
    d = h.shape[1]
    heads = d // (2 * HEAD_DIM)
    c_w = heads * 2 * HEAD_DIM
    t = lambda cols: cols // tn
    hn = _rmsnorm(h, g, RMS_EPS, BF16)
    w = w_in.astype(BF16)
    qk = _matmul(hn, w, t(2 * c_w), lambda j: j, BF16, "rope", (cos, sin),
                 scaled=(0, t(c_w)), factor=SOFTMAX_FACTOR)
    v = _matmul(hn, w, t(c_w), lambda j: t(2 * c_w) + j, BF16)
    z = _matmul(hn, w, t(c_w), lambda j: t(3 * c_w) + j, F32)
    out = _diff_attention(qk, v, z, lq1, lk1, lq2, lk2, subln_g, lambda_init, heads)
    return _matmul(out, w_out.astype(BF16), t(d), lambda j: j, F32, "residual", (h,))


def kernel(x, norm_g, w_in_even, sink_a, rpb_b, w_out_even, w_in_odd, lambda_q1, lambda_k1,
           lambda_q2, lambda_k2, subln_g, w_out_odd, final_g):
    b, s, d = x.shape
    depth = norm_g.shape[0]
    cos, sin = _rope_tables(s, HEAD_DIM)
    outs = []
    for bi in range(b):
        h = x.reshape(s, d) if b == 1 else x[bi]
        for layer in range(depth):
            i = layer // 2
            if layer % 2 == 0:
                h = _even_layer(h, norm_g[layer], w_in_even[i], sink_a[i], rpb_b[i],
                                w_out_even[i], cos, sin)
            else:
                lambda_init = 0.8 - 0.6 * math.exp(-0.3 * layer)
                h = _odd_layer(h, norm_g[layer], w_in_odd[i], lambda_q1[i], lambda_k1[i],
                               lambda_q2[i], lambda_k2[i], subln_g[i], w_out_odd[i],
                               cos, sin, lambda_init)
        outs.append(_rmsnorm(h, final_g, RMS_EPS, x.dtype))
    return outs[0].reshape(b, s, d) if b == 1 else jnp.stack(outs, axis=0)
```

```python
import functools
import math

import jax
import jax.numpy as jnp
import numpy as np
from jax import lax
from jax.experimental import pallas as pl
from jax.experimental.pallas import tpu as pltpu

F32 = jnp.float32
BF16 = jnp.bfloat16

HEAD_DIM = 128
ROPE_THETA = 10000.0
RMS_EPS = 1e-6
SUBLN_EPS = 1e-5
A_WINDOW = 128
A_BLOCK = 128
A_GROUP = 4
GRID_W = 64
NA_ROWS_MAX = 8
NA_COLS = 16
NA_Q_ROWS = 4
NA_K_ROWS = NA_Q_ROWS + NA_ROWS_MAX
NA_HEADS_PER_STEP = 4
MASK_VALUE = -1e30
BF16_ROWS = 16
LOG2E = math.log2(math.e)
SOFTMAX_FACTOR = HEAD_DIM ** -0.5 * LOG2E

VMEM_LIMIT_BYTES = 48 * 1024 * 1024
DIFF_VMEM_LIMIT_BYTES = 58 * 1024 * 1024
MATMUL_VMEM_LIMIT_BYTES = 60 * 1024 * 1024
MATMUL_TM = 2048
MATMUL_TN = 256

_NT_DIMS = (((1,), (1,)), ((), ()))


def _silu(z):
    return z * (1.0 / (1.0 + jnp.exp(-z)))


def _params(*semantics, vmem_limit_bytes=VMEM_LIMIT_BYTES, flags=None):
    return pltpu.CompilerParams(dimension_semantics=semantics,
                                vmem_limit_bytes=vmem_limit_bytes, flags=flags)


def _rmsnorm_kernel(x_ref, g_ref, o_ref, *, eps):
    x = x_ref[...]
    ms = jnp.mean(x * x, axis=-1, keepdims=True)
    o_ref[...] = (x * lax.rsqrt(ms + eps) * g_ref[...]).astype(o_ref.dtype)


def _rmsnorm(x, g, eps, out_dtype, tm=256):
    m, d = x.shape
    return pl.pallas_call(
        functools.partial(_rmsnorm_kernel, eps=eps),
        grid=(m // tm,),
        in_specs=[pl.BlockSpec((tm, d), lambda i: (i, 0)),
                  pl.BlockSpec((1, d), lambda i: (0, 0))],
        out_specs=pl.BlockSpec((tm, d), lambda i: (i, 0)),
        out_shape=jax.ShapeDtypeStruct((m, d), out_dtype),
        compiler_params=_params("arbitrary"),
        name="rmsnorm",
    )(x, g.reshape(1, d).astype(F32))


def _tile_factor(scaled, factor):
    j = pl.program_id(1)
    return jnp.where((j >= scaled[0]) & (j < scaled[1]), factor, 1.0).astype(F32)


def _mm_kernel(*refs, mode, scaled, factor):
    if mode == "rope":
        x_ref, w_ref, cos_ref, sin_ref, o_ref = refs
    elif mode == "residual":
        x_ref, w_ref, r_ref, o_ref = refs
    else:
        x_ref, w_ref, o_ref = refs
    acc = jnp.dot(x_ref[...], w_ref[...].astype(x_ref.dtype), preferred_element_type=F32)
    if mode == "rope":
        cos = cos_ref[...]
        sin = sin_ref[...]
        if scaled:
            cos = cos * _tile_factor(scaled, factor)
            sin = sin * _tile_factor(scaled, factor)
        hd = cos.shape[1]
        for c in range(acc.shape[1] // hd):
            xc = acc[:, c * hd:(c + 1) * hd]
            rot = pltpu.roll(xc, hd // 2, axis=1)
            o_ref[:, c * hd:(c + 1) * hd] = (xc * cos + rot * sin).astype(o_ref.dtype)
    elif mode == "residual":
        o_ref[...] = (r_ref[...] + acc).astype(o_ref.dtype)
    elif mode == "transposed":
        o_ref[...] = acc.T.astype(o_ref.dtype)
    else:
        if scaled:
            acc = acc * _tile_factor(scaled, factor)
        o_ref[...] = acc.astype(o_ref.dtype)


def _matmul(x, w, n_tiles, w_tile_of, out_dtype, mode="plain", extra=(), scaled=None,
            factor=1.0, tm=MATMUL_TM, tn=MATMUL_TN):
    m, k = x.shape
    if mode == "transposed":
        out_spec = pl.BlockSpec((tn, tm), lambda i, j: (j, i))
        out_shape = jax.ShapeDtypeStruct((n_tiles * tn, m), out_dtype)
    else:
        out_spec = pl.BlockSpec((tm, tn), lambda i, j: (i, j))
        out_shape = jax.ShapeDtypeStruct((m, n_tiles * tn), out_dtype)
    in_specs = [pl.BlockSpec((tm, k), lambda i, j: (i, 0)),
                pl.BlockSpec((k, tn), lambda i, j: (0, w_tile_of(j)))]
    if mode == "rope":
        hd = extra[0].shape[1]
        in_specs += [pl.BlockSpec((tm, hd), lambda i, j: (i, 0))] * 2
    elif mode == "residual":
        in_specs += [pl.BlockSpec((tm, tn), lambda i, j: (i, j))]
    return pl.pallas_call(
        functools.partial(_mm_kernel, mode=mode, scaled=scaled, factor=factor),
        grid=(m // tm, n_tiles),
        in_specs=in_specs,
        out_specs=out_spec,
        out_shape=out_shape,
        compiler_params=_params("arbitrary", "arbitrary", vmem_limit_bytes=MATMUL_VMEM_LIMIT_BYTES),
        name="matmul_" + mode,
    )(x, w, *extra)


def _window_kernel(sink_ref, mask_ref, q_ref, kp_ref, kc_ref, kn_ref, vp_ref, vc_ref, vn_ref,
                   z_ref, o_ref):
    hd = HEAD_DIM
    blk = q_ref.shape[0]
    kv_heads = kc_ref.shape[1] // hd
    mask = jnp.concatenate([mask_ref[...]] * A_GROUP, axis=0)

    def head_cols(head):
        return slice(head * hd, (head + 1) * hd)

    def window(refs, kvh):
        return jnp.concatenate([r[:, head_cols(kvh)] for r in refs], axis=0)

    scores = []
    for kvh in range(kv_heads):
        q = jnp.concatenate([q_ref[:, head_cols(kvh * A_GROUP + gi)] for gi in range(A_GROUP)],
                            axis=0)
        k = window((kp_ref, kc_ref, kn_ref), kvh)
        scores.append(lax.dot_general(q, k, _NT_DIMS, preferred_element_type=F32) + mask)
    probs, denoms = [], []
    for kvh, s in enumerate(scores):
        ps, ds = [], []
        for gi in range(A_GROUP):
            sg = s[gi * blk:(gi + 1) * blk]
            sink = sink_ref[kvh * A_GROUP + gi] * LOG2E
            m = jnp.maximum(jnp.max(sg, axis=-1, keepdims=True), sink)
            p = jnp.exp2(sg - m)
            ds.append(jnp.sum(p, axis=-1, keepdims=True) + jnp.exp2(sink - m))
            ps.append(p.astype(BF16))
        probs.append(jnp.concatenate(ps, axis=0))
        denoms.append(ds)
    for kvh, (p, ds) in enumerate(zip(probs, denoms)):
        v = window((vp_ref, vc_ref, vn_ref), kvh)
        o = jnp.dot(p, v, preferred_element_type=F32)
        for gi in range(A_GROUP):
            cols = head_cols(kvh * A_GROUP + gi)
            og = o[gi * blk:(gi + 1) * blk] / ds[gi]
            o_ref[:, cols] = (og * _silu(z_ref[:, cols])).astype(o_ref.dtype)


def _window_masks(seq):
    blk = A_BLOCK
    qpos = np.arange(blk)[:, None]
    kpos = np.arange(-blk, 2 * blk)[None, :]
    band = np.abs(qpos - kpos) <= A_WINDOW
    has_prev = np.broadcast_to(kpos >= 0, band.shape)
    has_next = np.broadcast_to(kpos < blk, band.shape)
    valid = np.stack([band & has_prev, band, band & has_next])
    if seq == blk:
        valid = valid & has_prev & has_next
    return np.where(valid, 0.0, MASK_VALUE).astype(np.float32)


def _window_attention(qk, vqkv, z, sink, kv_heads):
    seq = qk.shape[0]
    hd, blk = HEAD_DIM, A_BLOCK
    nb = seq // blk
    q_heads = kv_heads * A_GROUP
    kv_w, q_w = kv_heads * hd, q_heads * hd
    kspec = lambda f: pl.BlockSpec((blk, kv_w), lambda n: (f(n), q_w // kv_w))
    vspec = lambda f: pl.BlockSpec((blk, kv_w), lambda n: (f(n), 0))
    prev = lambda n: jnp.maximum(n - 1, 0)
    cur = lambda n: n
    nxt = lambda n: jnp.minimum(n + 1, nb - 1)
    wide = pl.BlockSpec((blk, q_w), lambda n: (n, 0))
    variant = lambda n: jnp.where(n == 0, 0, jnp.where(n == nb - 1, 2, 1))
    return pl.pallas_call(
        _window_kernel,
        grid=(nb,),
        in_specs=[pl.BlockSpec(memory_space=pltpu.SMEM),
                  pl.BlockSpec((None, blk, 3 * blk), lambda n: (variant(n), 0, 0)), wide,
                  kspec(prev), kspec(cur), kspec(nxt),
                  vspec(prev), vspec(cur), vspec(nxt), wide],
        out_specs=wide,
        out_shape=jax.ShapeDtypeStruct((seq, q_w), BF16),
        compiler_params=_params("arbitrary"),
        name="window_attention",
    )(sink.astype(F32), _window_masks(seq), qk, qk, qk, qk, vqkv, vqkv, vqkv, z)


def _na_kernel(q_ref, k0_ref, k1_ref, k2_ref, v0_ref, v1_ref, v2_ref, t_ref, z_ref, o_ref, b_scr,
               *, row_tables, groups):
    hd = HEAD_DIM
    n_heads = q_ref.shape[1] // hd
    g = pl.program_id(1)
    row_delta, row_valid = row_tables

    def build_bias(variant):
        for hh in range(n_heads):
            for i in range(NA_Q_ROWS):
                for j in range(NA_K_ROWS):
                    if row_valid[variant][i][j]:
                        tile = t_ref[hh, row_delta[variant][i][j] + NA_ROWS_MAX - 1]
                    else:
                        tile = jnp.full((GRID_W, GRID_W), MASK_VALUE, F32)
                    b_scr[hh, i * GRID_W:(i + 1) * GRID_W, j * GRID_W:(j + 1) * GRID_W] = tile

    for variant, at_group in enumerate((0, 1, groups - 1)):
        pl.when(g == at_group)(functools.partial(build_bias, variant))

    def head_cols(hh):
        return slice(hh * hd, (hh + 1) * hd)

    def window(refs, hh):
        return jnp.concatenate([r[:, head_cols(hh)] for r in refs], axis=0)

    scores = [lax.dot_general(q_ref[:, head_cols(hh)], window((k0_ref, k1_ref, k2_ref), hh),
                              _NT_DIMS, preferred_element_type=F32) + b_scr[hh]
              for hh in range(n_heads)]
    probs, denoms = [], []
    for s in scores:
        p = jnp.exp2(s - jnp.max(s, axis=-1, keepdims=True))
        denoms.append(jnp.sum(p, axis=-1, keepdims=True))
        probs.append(p.astype(BF16))
    for hh, (p, denom) in enumerate(zip(probs, denoms)):
        o = jnp.dot(p, window((v0_ref, v1_ref, v2_ref), hh), preferred_element_type=F32) / denom
        cols = head_cols(hh)
        o_ref[:, cols] = (o * _silu(z_ref[:, cols])).astype(o_ref.dtype)


def _na_row_tables(rows):
    kr = min(NA_ROWS_MAX, rows)
    groups = rows // NA_Q_ROWS
    variants, variant_of = [], []
    for g in range(groups):
        rq = NA_Q_ROWS * g + np.arange(NA_Q_ROWS)[:, None]
        key_row0 = int(np.clip(NA_Q_ROWS * g - NA_ROWS_MAX // 2, 0, rows - NA_K_ROWS))
        rk = key_row0 + np.arange(NA_K_ROWS)[None, :]
        rs = np.clip(rq - kr // 2, 0, rows - kr)
        valid = (rk >= rs) & (rk < rs + kr)
        assert (valid.sum(axis=1) == kr).all()
        dr = np.where(valid, rk - rq, 0)
        key = (dr.tobytes(), valid.tobytes())
        for idx, (other, _, _) in enumerate(variants):
            if other == key:
                variant_of.append(idx)
                break
        else:
            variant_of.append(len(variants))
            variants.append((key, dr, valid))
    assert variant_of == [0] + [1] * (groups - 2) + [2], variant_of
    dr = np.stack([v[1] for v in variants])
    valid = np.stack([v[2] for v in variants])
    return dr, valid


def _na_column_bias(rpb):
    col = np.arange(GRID_W)
    col_start = np.clip(col - NA_COLS // 2, 0, GRID_W - NA_COLS)
    valid_c = (col[None, :] >= col_start[:, None]) & (col[None, :] < col_start[:, None] + NA_COLS)
    dc = np.where(valid_c, col[None, :] - col[:, None], 0) + NA_COLS - 1
    select = (dc[None] == np.arange(rpb.shape[2])[:, None, None]).astype(np.float32)
    table = jnp.einsum("hdk,kcj->hdcj", rpb.astype(F32), select, precision=lax.Precision.HIGHEST)
    return jnp.where(valid_c[None, None], table, MASK_VALUE)


def _neighborhood_attention(vqkv, z, rpb, heads, col0):
    seq = vqkv.shape[0]
    hd = HEAD_DIM
    rows = seq // GRID_W
    groups = rows // NA_Q_ROWS
    tq = NA_Q_ROWS * GRID_W
    tk = NA_K_ROWS * GRID_W
    table = _na_column_bias(rpb) * LOG2E
    row_tables = tuple(a.tolist() for a in _na_row_tables(rows))
    assert groups >= 3
    hps = NA_HEADS_PER_STEP
    wd = hps * hd
    c0, steps = col0 // hps, heads // hps
    assert NA_K_ROWS == 3 * NA_Q_ROWS and NA_ROWS_MAX // 2 == NA_Q_ROWS
    first_block = lambda g: jnp.clip(g - 1, 0, groups - 3)
    kv_specs = [pl.BlockSpec((tq, wd), lambda h, g, i=i, c=c: (first_block(g) + i, c + h))
                for c in (c0 + steps, c0 + 2 * steps) for i in range(3)]
    return pl.pallas_call(
        functools.partial(_na_kernel, row_tables=row_tables, groups=groups),
        grid=(steps, groups),
        in_specs=[pl.BlockSpec((tq, wd), lambda h, g: (g, c0 + h))] + kv_specs + [
                  pl.BlockSpec((hps,) + table.shape[1:], lambda h, g: (h, 0, 0, 0)),
                  pl.BlockSpec((tq, wd), lambda h, g: (g, steps + h))],
        out_specs=pl.BlockSpec((tq, wd), lambda h, g: (g, h)),
        out_shape=jax.ShapeDtypeStruct((seq, heads * hd), BF16),
        scratch_shapes=[pltpu.VMEM((hps, tq, tk), F32)],
        compiler_params=_params("arbitrary", "arbitrary"),
        name="neighborhood_attention",
    )(*([vqkv] * 7), table, z)


def _diff_kernel(lq1_ref, lk1_ref, lq2_ref, lk2_ref, g_ref, q_ref, k_ref, vt_ref, z_ref, o_ref,
                 s_scr, p_scr, acc_scr, m_scr, alpha_scr, *, tk, lambda_init):
    width, seq = vt_ref.shape
    ones_rows = acc_scr.shape[1] - width
    dc = width // 2
    n_chunks = seq // tk
    q_parts = (q_ref[:, :dc], q_ref[:, dc:])

    def scores(c, slot):
        kc = k_ref[c * tk:(c + 1) * tk, :]
        for t in range(2):
            s_scr[slot, t] = lax.dot_general(kc[:, t * dc:(t + 1) * dc], q_parts[t], _NT_DIMS,
                                             preferred_element_type=F32)

    def softmax_chunk(slot, first):
        for t in range(2):
            s = s_scr[slot, t]
            m_new = jnp.max(s, axis=0, keepdims=True)
            if not first:
                m_old = m_scr[t]
                m_new = jnp.maximum(m_old, m_new)
            if not first:
                alpha_scr[slot, t] = jnp.exp2(m_old - m_new)
            m_scr[t] = m_new
            p_scr[slot, t] = jnp.exp2(s - m_new).astype(BF16)

    def accumulate(slot, c, first):
        lhs = jnp.concatenate([vt_ref[:, c * tk:(c + 1) * tk], jnp.ones((ones_rows, tk), BF16)],
                              axis=0)
        out = []
        for t in range(2):
            pv = jnp.dot(lhs, p_scr[slot, t], preferred_element_type=F32)
            if not first:
                pv = alpha_scr[slot, t] * acc_scr[t] + pv
            out.append(pv)
        return out

    scores(0, 0)
    for c in range(n_chunks):
        if c >= 1:
            for t, acc in enumerate(accumulate((c - 1) % 2, c - 1, first=(c == 1))):
                acc_scr[t] = acc
        if c + 1 < n_chunks:
            scores(c + 1, (c + 1) % 2)
        softmax_chunk(c % 2, first=(c == 0))

    last = n_chunks - 1
    outs = [(acc[:width] / acc[width:width + 1]).T
            for acc in accumulate(last % 2, last, first=(last == 0))]

    lam = (jnp.exp(jnp.sum(lq1_ref[...] * lk1_ref[...], axis=-1, keepdims=True))
           - jnp.exp(jnp.sum(lq2_ref[...] * lk2_ref[...], axis=-1, keepdims=True))
           + lambda_init)
    o = outs[0] - lam * outs[1]
    ms = jnp.mean(o * o, axis=-1, keepdims=True)
    y = o * lax.rsqrt(ms + SUBLN_EPS) * g_ref[...]
    y = y * (1.0 - lambda_init)
    o_ref[...] = (y * _silu(z_ref[...])).astype(o_ref.dtype)


def _diff_attention(qk, vt, z, lq1, lk1, lq2, lk2, subln_g, lambda_init, heads, tq=512, tk=1024):
    seq = qk.shape[0]
    width = 2 * HEAD_DIM
    row = lambda a: a.reshape(1, -1).astype(F32)
    small = lambda n: pl.BlockSpec((1, n), lambda h, i: (0, 0))
    return pl.pallas_call(
        functools.partial(_diff_kernel, tk=tk, lambda_init=lambda_init),
        grid=(heads, seq // tq),
        in_specs=[small(HEAD_DIM)] * 4 + [small(width)] + [
            pl.BlockSpec((tq, width), lambda h, i: (i, h)),
            pl.BlockSpec((seq, width), lambda h, i: (0, heads + h)),
            pl.BlockSpec((width, seq), lambda h, i: (h, 0)),
            pl.BlockSpec((tq, width), lambda h, i: (i, h))],
        out_specs=pl.BlockSpec((tq, width), lambda h, i: (i, h)),
        out_shape=jax.ShapeDtypeStruct((seq, heads * width), BF16),
        scratch_shapes=[pltpu.VMEM((2, 2, tk, tq), F32),
                        pltpu.VMEM((2, 2, tk, tq), BF16),
                        pltpu.VMEM((2, width + BF16_ROWS, tq), F32),
                        pltpu.VMEM((2, 1, tq), F32),
                        pltpu.VMEM((2, 2, 1, tq), F32)],
        compiler_params=_params("arbitrary", "arbitrary", vmem_limit_bytes=DIFF_VMEM_LIMIT_BYTES),
        name="diff_attention",
    )(row(lq1), row(lk1), row(lq2), row(lk2), row(subln_g), qk, qk, vt, z)


def _rope_tables(seq, dim):
    pos = jnp.arange(seq, dtype=F32)
    inv = 1.0 / (ROPE_THETA ** (jnp.arange(0, dim, 2, dtype=F32) / dim))
    ang = pos[:, None] * inv[None, :]
    ang = jnp.concatenate([ang, ang], axis=-1)
    sign = jnp.where(jnp.arange(dim) < dim // 2, -1.0, 1.0).astype(F32)
    return jnp.cos(ang), jnp.sin(ang) * sign[None, :]


def _even_layer(h, g, w_in, sink, rpb, w_out, cos, sin, tn=MATMUL_TN):
    d = h.shape[1]
    a_heads = d // (2 * HEAD_DIM)
    kv_heads = a_heads // A_GROUP
    b_heads = d // (2 * HEAD_DIM)
    a_q, a_kv, b_w = a_heads * HEAD_DIM, kv_heads * HEAD_DIM, b_heads * HEAD_DIM
    t = lambda cols: cols // tn
    o_va, o_za = t(a_q + a_kv), t(a_q + 2 * a_kv)
    o_qb = t(2 * a_q + 2 * a_kv)
    o_zb = o_qb + t(3 * b_w)
    hn = _rmsnorm(h, g, RMS_EPS, BF16)
    qk = _matmul(hn, w_in, t(a_q + a_kv), lambda j: j, BF16, "rope", (cos, sin),
                 scaled=(0, t(a_q)), factor=SOFTMAX_FACTOR)
    vqkv = _matmul(hn, w_in, t(a_kv + 3 * b_w),
                   lambda j: jnp.where(j < t(a_kv), o_va + j, o_qb + j - t(a_kv)), BF16,
                   scaled=(t(a_kv), t(a_kv + b_w)), factor=SOFTMAX_FACTOR)
    z = _matmul(hn, w_in, t(a_q + b_w),
                lambda j: jnp.where(j < t(a_q), o_za + j, o_zb + j - t(a_q)), F32)
    out_a = _window_attention(qk, vqkv, z, sink, kv_heads)
    out_b = _neighborhood_attention(vqkv, z, rpb, b_heads, a_kv // HEAD_DIM)
    mixed = jnp.concatenate([out_a, out_b], axis=-1)
    return _matmul(mixed, w_out, t(d), lambda j: j, F32, "residual", (h,))


def _odd_layer(h, g, w_in, lq1, lk1, lq2, lk2, subln_g, w_out, cos, sin, lambda_init,
               tn=MATMUL_TN):
    d = h.shape[1]
    heads = d // (2 * HEAD_DIM)
    c_w = heads * 2 * HEAD_DIM
    t = lambda cols: cols // tn
    hn = _rmsnorm(h, g, RMS_EPS, BF16)
    qk = _matmul(hn, w_in, t(2 * c_w), lambda j: j, BF16, "rope", (cos, sin),
                 scaled=(0, t(c_w)), factor=SOFTMAX_FACTOR)
    vt = _matmul(hn, w_in, t(c_w), lambda j: t(2 * c_w) + j, BF16, "transposed")
    z = _matmul(hn, w_in, t(c_w), lambda j: t(3 * c_w) + j, F32)
    out = _diff_attention(qk, vt, z, lq1, lk1, lq2, lk2, subln_g, lambda_init, heads)
    return _matmul(out, w_out, t(d), lambda j: j, F32, "residual", (h,))


def kernel(x, norm_g, w_in_even, sink_a, rpb_b, w_out_even, w_in_odd, lambda_q1, lambda_k1,
           lambda_q2, lambda_k2, subln_g, w_out_odd, final_g):
    b, s, d = x.shape
    depth = norm_g.shape[0]
    cos, sin = _rope_tables(s, HEAD_DIM)
    outs = []
    for bi in range(b):
        h = x.reshape(s, d) if b == 1 else x[bi]
        for layer in range(depth):
            i = layer // 2
            if layer % 2 == 0:
                h = _even_layer(h, norm_g[layer], w_in_even[i], sink_a[i], rpb_b[i],
                                w_out_even[i], cos, sin)
            else:
                lambda_init = 0.8 - 0.6 * math.exp(-0.3 * layer)
                h = _odd_layer(h, norm_g[layer], w_in_odd[i], lambda_q1[i], lambda_k1[i],
                               lambda_q2[i], lambda_k2[i], subln_g[i], w_out_odd[i],
                               cos, sin, lambda_init)
        outs.append(_rmsnorm(h, final_g, RMS_EPS, x.dtype))
    return outs[0].reshape(b, s, d) if b == 1 else jnp.stack(outs, axis=0)
```
